```python
import math
import jax, jax.numpy as jnp
from jax import lax
import numpy as np

D_MODEL = 1024
BATCH = 4
SEQ = 4096
DEPTH = 1

CHUNK = 64
D_MIX = 2 * D_MODEL
D_SSM = D_MIX // 2
D_CONV = D_MIX - D_SSM
SSM_HEAD_DIM = 64
SSM_HEADS = D_SSM // SSM_HEAD_DIM
SSM_GROUPS = 2
SSM_STATE = 128
SSM_CONV_W = 4
SSM_XBC = D_SSM + 2 * SSM_GROUPS * SSM_STATE
CONF_KERNEL = 31
D_FF = 2816
D_IN_PROJ = D_SSM + SSM_XBC + SSM_HEADS + 2 * D_CONV
EPS = 1e-6

kernel_name = 'hybrid_ssd_conformer_conv_macaron_block'


def rms_norm(x, g):
    xf = x.astype(jnp.float32)
    y = xf * lax.rsqrt(jnp.mean(xf * xf, axis=-1, keepdims=True) + EPS)
    return (y * g.astype(jnp.float32)).astype(x.dtype)


def layer_norm(x, g, b):
    xf = x.astype(jnp.float32)
    mu = jnp.mean(xf, axis=-1, keepdims=True)
    var = jnp.mean(jnp.square(xf - mu), axis=-1, keepdims=True)
    y = (xf - mu) * lax.rsqrt(var + EPS)
    return (y * g.astype(jnp.float32) + b.astype(jnp.float32)).astype(x.dtype)


def causal_depthwise_conv(x, w, b):
    k = w.shape[0]
    y = lax.conv_general_dilated(
        x, w[:, None, :], window_strides=(1,), padding=[(k - 1, 0)],
        dimension_numbers=('NWC', 'WIO', 'NWC'), feature_group_count=x.shape[-1])
    return y + b


def swiglu(h, w_gate, w_up, w_down):
    return (jax.nn.silu(h @ w_gate) * (h @ w_up)) @ w_down


def ssd_chunked(xh, dt, a, bm, cm):
    b, seq = xh.shape[:2]
    nc = seq // CHUNK
    g, r, p, n = SSM_GROUPS, SSM_HEADS // SSM_GROUPS, SSM_HEAD_DIM, SSM_STATE
    f32 = jnp.float32
    xc = (xh.astype(f32) * dt[..., None]).reshape(b, nc, CHUNK, g, r, p)
    bc = bm.astype(f32).reshape(b, nc, CHUNK, g, n)
    cc = cm.astype(f32).reshape(b, nc, CHUNK, g, n)
    da = (dt * a).reshape(b, nc, CHUNK, g, r).transpose(0, 1, 3, 4, 2)
    a_cs = jnp.cumsum(da, axis=-1)
    causal = jnp.tril(jnp.ones((CHUNK, CHUNK), dtype=bool))
    seg = a_cs[..., :, None] - a_cs[..., None, :]
    decay_ls = jnp.exp(jnp.where(causal, seg, -jnp.inf))
    cb = jnp.einsum('bclgn,bcsgn->bcgls', cc, bc)
    y_diag = jnp.einsum('bcgls,bcgrls,bcsgrp->bclgrp', cb, decay_ls, xc)
    decay_to_end = jnp.exp(a_cs[..., -1:] - a_cs)
    states = jnp.einsum('bclgn,bcgrl,bclgrp->bcgrpn', bc, decay_to_end, xc)
    chunk_decay = jnp.exp(a_cs[..., -1])

    def step(h, inp):
        s, d = inp
        return h * d[..., None, None] + s, h

    h0 = jnp.zeros((b, g, r, p, n), f32)
    _, prev = lax.scan(step, h0, (jnp.swapaxes(states, 0, 1), jnp.swapaxes(chunk_decay, 0, 1)))
    prev = jnp.swapaxes(prev, 0, 1)
    y_off = jnp.einsum('bclgn,bcgrpn,bcgrl->bclgrp', cc, prev, jnp.exp(a_cs))
    return (y_diag + y_off).reshape(b, seq, SSM_HEADS, p)


def hybrid_mixer(h, w_in, b_glu, ssd_conv_w, ssd_conv_b, dt_bias, a_log, d_skip, ssd_norm_g,
                 conf_dw_w, conf_dw_b, conf_ln_g, conf_ln_b, w_out):
    b, seq, _ = h.shape
    f32 = jnp.float32
    proj = h @ w_in
    z, xbc, dt_raw, glu = jnp.split(
        proj, [D_SSM, D_SSM + SSM_XBC, D_SSM + SSM_XBC + SSM_HEADS], axis=-1)
    xbc = jax.nn.silu(causal_depthwise_conv(xbc, ssd_conv_w, ssd_conv_b))
    xs, bm, cm = jnp.split(xbc, [D_SSM, D_SSM + SSM_GROUPS * SSM_STATE], axis=-1)
    dt = jax.nn.softplus(dt_raw.astype(f32) + dt_bias.astype(f32))
    a = -jnp.exp(a_log.astype(f32))
    xh = xs.reshape(b, seq, SSM_HEADS, SSM_HEAD_DIM)
    y = ssd_chunked(xh, dt, a,
                    bm.reshape(b, seq, SSM_GROUPS, SSM_STATE),
                    cm.reshape(b, seq, SSM_GROUPS, SSM_STATE))
    y = y + d_skip.astype(f32)[:, None] * xh.astype(f32)
    y = y.reshape(b, seq, D_SSM) * jax.nn.silu(z.astype(f32))
    yg = y.reshape(b, seq, SSM_GROUPS, D_SSM // SSM_GROUPS)
    yg = yg * lax.rsqrt(jnp.mean(yg * yg, axis=-1, keepdims=True) + EPS)
    y_ssd = (yg.reshape(b, seq, D_SSM) * ssd_norm_g.astype(f32)).astype(h.dtype)
    u_a, u_g = jnp.split(glu + b_glu, 2, axis=-1)
    u = u_a * jax.nn.sigmoid(u_g)
    u = causal_depthwise_conv(u, conf_dw_w, conf_dw_b)
    y_conv = jax.nn.silu(layer_norm(u, conf_ln_g, conf_ln_b))
    return jnp.concatenate([y_ssd, y_conv], axis=-1) @ w_out


def setup_inputs(seed: int = 0) -> dict:
    key = jax.random.key(seed)
    ks = jax.random.split(key, 32)
    f32 = jnp.float32

    def nrm(k, shape, scale):
        return jax.random.normal(k, shape, f32) * scale

    def gain(k, shape):
        return 1.0 + 0.05 * jax.random.normal(k, shape, f32)

    L = DEPTH
    dt0 = jnp.exp(jax.random.uniform(ks[10], (L, SSM_HEADS), f32)
                  * (math.log(0.1) - math.log(0.001)) + math.log(0.001))
    return {
        'x': jax.random.normal(ks[0], (BATCH, SEQ, D_MODEL), f32),
        'ffn1_pre_g': gain(ks[1], (L, D_MODEL)),
        'ffn1_w_gate': nrm(ks[2], (L, D_MODEL, D_FF), D_MODEL ** -0.5),
        'ffn1_w_up': nrm(ks[3], (L, D_MODEL, D_FF), D_MODEL ** -0.5),
        'ffn1_w_down': nrm(ks[4], (L, D_FF, D_MODEL), D_FF ** -0.5),
        'ffn1_post_g': gain(ks[5], (L, D_MODEL)),
        'mix_pre_g': gain(ks[6], (L, D_MODEL)),
        'w_in': nrm(ks[7], (L, D_MODEL, D_IN_PROJ), D_MODEL ** -0.5),
        'b_glu': nrm(ks[8], (L, 2 * D_CONV), 0.01),
        'ssd_conv_w': nrm(ks[9], (L, SSM_CONV_W, SSM_XBC), SSM_CONV_W ** -0.5),
        'ssd_conv_b': nrm(ks[11], (L, SSM_XBC), 0.01),
        'dt_bias': dt0 + jnp.log(-jnp.expm1(-dt0)),
        'a_log': jnp.log(jax.random.uniform(ks[12], (L, SSM_HEADS), f32, 1.0, 16.0)),
        'd_skip': 1.0 + 0.1 * jax.random.normal(ks[13], (L, SSM_HEADS), f32),
        'ssd_norm_g': gain(ks[14], (L, D_SSM)),
        'conf_dw_w': nrm(ks[15], (L, CONF_KERNEL, D_CONV), CONF_KERNEL ** -0.5),
        'conf_dw_b': nrm(ks[16], (L, D_CONV), 0.01),
        'conf_ln_g': gain(ks[17], (L, D_CONV)),
        'conf_ln_b': nrm(ks[18], (L, D_CONV), 0.01),
        'w_out': nrm(ks[19], (L, D_MIX, D_MODEL), D_MIX ** -0.5),
        'mix_post_g': gain(ks[20], (L, D_MODEL)),
        'ffn2_pre_g': gain(ks[21], (L, D_MODEL)),
        'ffn2_w_gate': nrm(ks[22], (L, D_MODEL, D_FF), D_MODEL ** -0.5),
        'ffn2_w_up': nrm(ks[23], (L, D_MODEL, D_FF), D_MODEL ** -0.5),
        'ffn2_w_down': nrm(ks[24], (L, D_FF, D_MODEL), D_FF ** -0.5),
        'ffn2_post_g': gain(ks[25], (L, D_MODEL)),
    }


def reference(x, ffn1_pre_g, ffn1_w_gate, ffn1_w_up, ffn1_w_down, ffn1_post_g,
              mix_pre_g, w_in, b_glu, ssd_conv_w, ssd_conv_b, dt_bias, a_log, d_skip,
              ssd_norm_g, conf_dw_w, conf_dw_b, conf_ln_g, conf_ln_b, w_out, mix_post_g,
              ffn2_pre_g, ffn2_w_gate, ffn2_w_up, ffn2_w_down, ffn2_post_g):
    for i in range(DEPTH):
        f = swiglu(rms_norm(x, ffn1_pre_g[i]), ffn1_w_gate[i], ffn1_w_up[i], ffn1_w_down[i])
        x = x + 0.5 * rms_norm(f, ffn1_post_g[i])
        m = hybrid_mixer(rms_norm(x, mix_pre_g[i]), w_in[i], b_glu[i], ssd_conv_w[i],
                         ssd_conv_b[i], dt_bias[i], a_log[i], d_skip[i], ssd_norm_g[i],
                         conf_dw_w[i], conf_dw_b[i], conf_ln_g[i], conf_ln_b[i], w_out[i])
        x = x + rms_norm(m, mix_post_g[i])
        f = swiglu(rms_norm(x, ffn2_pre_g[i]), ffn2_w_gate[i], ffn2_w_up[i], ffn2_w_down[i])
        x = x + 0.5 * rms_norm(f, ffn2_post_g[i])
    return x
```

```python
import functools

import jax
import jax.numpy as jnp
from jax import lax
from jax.experimental import pallas as pl
from jax.experimental.pallas import tpu as pltpu

F32 = jnp.float32
BF16 = jnp.bfloat16

EPS = 1e-6
LANES = 128
VMEM_LIMIT_BYTES = 56 * 1024 * 1024

SSM_HEAD_DIM = 64
SSM_GROUPS = 2
SSM_STATE = 128
SSM_CONV_W = 4
CONF_KERNEL = 31

FFN_ROWS = 512
MIX_ROWS = 512
SSD_Q = 128
CONV_ROWS = 128
XBC_HALO = 8
U_HALO = 32


def _resident(shape):
    nd = len(shape)
    return pl.BlockSpec(shape, lambda *_: (0,) * nd, pipeline_mode=pl.Buffered(1))


def _rms(x, g):
    return x * lax.rsqrt(jnp.mean(x * x, axis=-1, keepdims=True) + EPS) * g


def _silu(x):
    return x * jax.nn.sigmoid(x)


def _dot(a, b):
    return jnp.dot(a, b, preferred_element_type=F32)


def _ffn_kernel(x_ref, pre_g_ref, wg_ref, wu_ref, wd_ref, post_g_ref, o_ref, *, f_chunk):
    x = x_ref[...]
    xn = _rms(x, pre_g_ref[...]).astype(BF16)
    d_ff = wg_ref.shape[1]
    acc = jnp.zeros(x.shape, F32)
    for c in range(d_ff // f_chunk):
        sl = slice(c * f_chunk, (c + 1) * f_chunk)
        g = _dot(xn, wg_ref[:, sl])
        u = _dot(xn, wu_ref[:, sl])
        acc = acc + _dot((_silu(g) * u).astype(BF16), wd_ref[sl, :])
    o_ref[...] = x + 0.5 * _rms(acc, post_g_ref[...])


def _ffn(x2, pre_g, w_gate, w_up, w_down, post_g):
    m, d = x2.shape
    d_ff = w_gate.shape[1]
    f_chunk = d_ff // 2 if (d_ff // 2) % LANES == 0 else d_ff
    row_spec = pl.BlockSpec((FFN_ROWS, d), lambda i: (i, 0))
    return pl.pallas_call(
        functools.partial(_ffn_kernel, f_chunk=f_chunk),
        grid=(m // FFN_ROWS,),
        in_specs=[row_spec, _resident((1, d)), _resident((d, d_ff)), _resident((d, d_ff)),
                  _resident((d_ff, d)), _resident((1, d))],
        out_specs=row_spec,
        out_shape=jax.ShapeDtypeStruct((m, d), F32),
        compiler_params=pltpu.CompilerParams(
            dimension_semantics=("arbitrary",), vmem_limit_bytes=VMEM_LIMIT_BYTES),
    )(x2, pre_g.reshape(1, d), w_gate.astype(BF16), w_up.astype(BF16), w_down.astype(BF16),
      post_g.reshape(1, d))


def _depthwise_conv(ext_ref, first_row, w_ref, b_ref, out_ref, n_rows):
    taps, channels = w_ref.shape
    for cb in range(channels // LANES):
        cs = slice(cb * LANES, (cb + 1) * LANES)
        for rb in range(n_rows // CONV_ROWS):
            acc = jnp.broadcast_to(b_ref[:, cs], (CONV_ROWS, LANES))
            for k in range(taps):
                acc = acc + w_ref[k:k + 1, cs] * ext_ref[pl.ds(first_row + rb * CONV_ROWS + k, CONV_ROWS), cs]
            out_ref[rb * CONV_ROWS:(rb + 1) * CONV_ROWS, cs] = acc


def _mixer_kernel(x_ref, pre_g_ref, w_in_ref, b_glu_ref, cw_ref, cbias_ref, dtb_ref, alog_ref,
                  dskip_ref, ng_ref, dww_ref, dwb_ref, lng_ref, lnb_ref, expand_ref, w_out_ref,
                  post_g_ref, o_ref,
                  xbc_ext, xbc_conv, u_ext, u_conv, state, y_ssd,
                  *, d_ssm, d_conv, n_heads):
    tl = x_ref.shape[0]
    q = SSD_Q
    gn = SSM_GROUPS * SSM_STATE
    d_xbc = d_ssm + 2 * gn
    grp_w = d_ssm // SSM_GROUPS
    heads_per_group = n_heads // SSM_GROUPS

    @pl.when(pl.program_id(1) == 0)
    def _():
        xbc_ext[0:XBC_HALO, :] = jnp.zeros((XBC_HALO, d_xbc), F32)
        u_ext[0:U_HALO, :] = jnp.zeros((U_HALO, d_conv), F32)
        state[...] = jnp.zeros(state.shape, F32)

    x = x_ref[...]
    xn = _rms(x, pre_g_ref[...]).astype(BF16)

    o_xbc = d_ssm
    o_dt = o_xbc + d_xbc
    o_ga = o_dt + LANES
    o_gg = o_ga + d_conv
    z = _dot(xn, w_in_ref[:, 0:d_ssm])
    xbc_ext[XBC_HALO:XBC_HALO + tl, :] = _dot(xn, w_in_ref[:, o_xbc:o_dt])
    dt_raw = _dot(xn, w_in_ref[:, o_dt:o_ga])
    glu_a = _dot(xn, w_in_ref[:, o_ga:o_gg]) + b_glu_ref[:, 0:d_conv]
    glu_g = _dot(xn, w_in_ref[:, o_gg:o_gg + d_conv]) + b_glu_ref[:, d_conv:2 * d_conv]
    u_ext[U_HALO:U_HALO + tl, :] = glu_a * jax.nn.sigmoid(glu_g)

    _depthwise_conv(xbc_ext, XBC_HALO - (SSM_CONV_W - 1), cw_ref, cbias_ref, xbc_conv, tl)
    xbc_ext[0:XBC_HALO, :] = xbc_ext[tl:tl + XBC_HALO, :]
    _depthwise_conv(u_ext, U_HALO - (CONF_KERNEL - 1), dww_ref, dwb_ref, u_conv, tl)
    u_ext[0:U_HALO, :] = u_ext[tl:tl + U_HALO, :]

    lane_c = lax.broadcasted_iota(jnp.int32, (1, LANES), 1)
    head_lane = lane_c < n_heads
    a_neg = jnp.where(head_lane, -jnp.exp(alog_ref[...]), 0.0)
    dt_c = jnp.where(head_lane, jax.nn.softplus(dt_raw + dtb_ref[...]), 0.0)
    dt_r = dt_c.T
    acs_r = (dt_c * a_neg).T
    pos = lax.broadcasted_iota(jnp.int32, acs_r.shape, 1) % q
    shift = 1
    while shift < q:
        acs_r = acs_r + jnp.where(pos >= shift, pltpu.roll(acs_r, shift, axis=1), 0.0)
        shift *= 2
    acs_c = acs_r.T

    tril = (lax.broadcasted_iota(jnp.int32, (q, q), 0) >= lax.broadcasted_iota(jnp.int32, (q, q), 1))
    left = lax.broadcasted_iota(jnp.int32, (q, LANES), 1) < SSM_HEAD_DIM
    expand = expand_ref[...]

    def expand_heads(v):
        hi = v.astype(BF16)
        lo = (v - hi.astype(F32)).astype(BF16)
        return _dot(hi, expand) + _dot(lo, expand)

    for c in range(tl // q):
        rows = slice(c * q, (c + 1) * q)
        acs_k = acs_c[rows, :]
        acs_last = acs_k[q - 1:q, :]
        decay_in = expand_heads(jnp.exp(acs_k))
        w_end = expand_heads(dt_c[rows, :] * jnp.exp(acs_last - acs_k))
        xbc_k = _silu(xbc_conv[rows, :])
        xs = xbc_k[:, 0:d_ssm]
        xs_bf = xs.astype(BF16)
        xw_bf = (xs * w_end).astype(BF16)
        zk = z[rows, :]
        for g in range(SSM_GROUPS):
            gs = slice(g * grp_w, (g + 1) * grp_w)
            b_g = xbc_k[:, d_ssm + g * SSM_STATE:d_ssm + (g + 1) * SSM_STATE].astype(BF16)
            c_g = xbc_k[:, d_ssm + gn + g * SSM_STATE:d_ssm + gn + (g + 1) * SSM_STATE].astype(BF16)
            cb = lax.dot_general(c_g, b_g, (((1,), (1,)), ((), ())), preferred_element_type=F32)
            st_g = state[:, gs]
            y_g = _dot(c_g, st_g.astype(BF16)) * decay_in[:, gs]
            y_pairs = []
            for hp in range(heads_per_group // 2):
                ms = []
                for h in (g * heads_per_group + 2 * hp, g * heads_per_group + 2 * hp + 1):
                    seg = jnp.broadcast_to(acs_k[:, h:h + 1], (q, q)) - acs_r[h:h + 1, rows]
                    dec = jnp.exp(jnp.where(tril, seg, -1e30))
                    ms.append((cb * dec * dt_r[h:h + 1, rows]).astype(BF16))
                pair = xs_bf[:, g * grp_w + hp * LANES:g * grp_w + (hp + 1) * LANES]
                zero = jnp.zeros_like(pair)
                rhs = jnp.concatenate([jnp.where(left, pair, zero), jnp.where(left, zero, pair)], axis=0)
                y_pairs.append(_dot(jnp.concatenate(ms, axis=1), rhs))
            y_g = y_g + jnp.concatenate(y_pairs, axis=1) + dskip_ref[:, gs] * xs[:, gs]
            y_g = y_g * _silu(zk[:, gs])
            y_g = y_g * lax.rsqrt(jnp.mean(y_g * y_g, axis=-1, keepdims=True) + EPS)
            y_ssd[rows, gs] = (y_g * ng_ref[:, gs]).astype(BF16)
            upd = lax.dot_general(b_g, xw_bf[:, gs], (((0,), (0,)), ((), ())), preferred_element_type=F32)
            state[:, gs] = st_g * decay_in[q - 1:q, gs] + upd

    uc = u_conv[...]
    mu = jnp.mean(uc, axis=-1, keepdims=True)
    ucc = uc - mu
    var = jnp.mean(ucc * ucc, axis=-1, keepdims=True)
    y_conv = _silu(ucc * lax.rsqrt(var + EPS) * lng_ref[...] + lnb_ref[...]).astype(BF16)

    m = _dot(y_ssd[...], w_out_ref[0:d_ssm, :]) + _dot(y_conv, w_out_ref[d_ssm:d_ssm + d_conv, :])
    o_ref[...] = x + _rms(m, post_g_ref[...])


def _mixer(x, pre_g, w_in, b_glu, ssd_conv_w, ssd_conv_b, dt_bias, a_log, d_skip, ssd_norm_g,
           conf_dw_w, conf_dw_b, conf_ln_g, conf_ln_b, w_out, post_g):
    b, seq, d = x.shape
    n_heads = a_log.shape[0]
    d_ssm = n_heads * SSM_HEAD_DIM
    d_conv = conf_dw_w.shape[1]
    gn = SSM_GROUPS * SSM_STATE
    d_xbc = d_ssm + 2 * gn
    assert n_heads <= LANES and SSD_Q == LANES and MIX_ROWS % SSD_Q == 0 and seq % MIX_ROWS == 0

    def row(v):
        return v.reshape(1, -1)

    def pad_lanes(v):
        return jnp.pad(v, (0, LANES - v.shape[0])).reshape(1, LANES)

    o_dt = d_ssm + d_xbc
    w_in_k = jnp.concatenate(
        [w_in[:, :o_dt], jnp.pad(w_in[:, o_dt:o_dt + n_heads], ((0, 0), (0, LANES - n_heads))),
         w_in[:, o_dt + n_heads:]], axis=1).astype(BF16)
    d_in_k = w_in_k.shape[1]
    expand = (jnp.arange(LANES)[:, None] == (jnp.arange(d_ssm) // SSM_HEAD_DIM)[None, :]).astype(BF16)
    dskip_x = jnp.repeat(d_skip, SSM_HEAD_DIM).reshape(1, d_ssm)

    tile = pl.BlockSpec((None, MIX_ROWS, d), lambda bi, ti: (bi, ti, 0))
    kern = functools.partial(_mixer_kernel, d_ssm=d_ssm, d_conv=d_conv, n_heads=n_heads)
    return pl.pallas_call(
        kern,
        grid=(b, seq // MIX_ROWS),
        in_specs=[tile, _resident((1, d)), _resident((d, d_in_k)), _resident((1, 2 * d_conv)),
                  _resident((SSM_CONV_W, d_xbc)), _resident((1, d_xbc)), _resident((1, LANES)),
                  _resident((1, LANES)), _resident((1, d_ssm)), _resident((1, d_ssm)),
                  _resident((CONF_KERNEL, d_conv)), _resident((1, d_conv)), _resident((1, d_conv)),
                  _resident((1, d_conv)), _resident((LANES, d_ssm)), _resident((d_ssm + d_conv, d)),
                  _resident((1, d))],
        out_specs=tile,
        out_shape=jax.ShapeDtypeStruct((b, seq, d), F32),
        scratch_shapes=[
            pltpu.VMEM((XBC_HALO + MIX_ROWS, d_xbc), F32),
            pltpu.VMEM((MIX_ROWS, d_xbc), F32),
            pltpu.VMEM((U_HALO + MIX_ROWS, d_conv), F32),
            pltpu.VMEM((MIX_ROWS, d_conv), F32),
            pltpu.VMEM((SSM_STATE, d_ssm), F32),
            pltpu.VMEM((MIX_ROWS, d_ssm), BF16),
        ],
        compiler_params=pltpu.CompilerParams(
            dimension_semantics=("arbitrary", "arbitrary"), vmem_limit_bytes=VMEM_LIMIT_BYTES),
    )(x, row(pre_g), w_in_k, row(b_glu), ssd_conv_w, row(ssd_conv_b), pad_lanes(dt_bias),
      pad_lanes(a_log), dskip_x, row(ssd_norm_g), conf_dw_w, row(conf_dw_b), row(conf_ln_g),
      row(conf_ln_b), expand, w_out.astype(BF16), row(post_g))


def kernel(x, ffn1_pre_g, ffn1_w_gate, ffn1_w_up, ffn1_w_down, ffn1_post_g, mix_pre_g, w_in, b_glu, ssd_conv_w, ssd_conv_b, dt_bias, a_log, d_skip, ssd_norm_g, conf_dw_w, conf_dw_b, conf_ln_g, conf_ln_b, w_out, mix_post_g, ffn2_pre_g, ffn2_w_gate, ffn2_w_up, ffn2_w_down, ffn2_post_g):
    b, seq, d = x.shape
    for i in range(ffn1_pre_g.shape[0]):
        x = _ffn(x.reshape(b * seq, d), ffn1_pre_g[i], ffn1_w_gate[i], ffn1_w_up[i], ffn1_w_down[i],
                 ffn1_post_g[i]).reshape(b, seq, d)
        x = _mixer(x, mix_pre_g[i], w_in[i], b_glu[i], ssd_conv_w[i], ssd_conv_b[i], dt_bias[i],
                   a_log[i], d_skip[i], ssd_norm_g[i], conf_dw_w[i], conf_dw_b[i], conf_ln_g[i],
                   conf_ln_b[i], w_out[i], mix_post_g[i])
        x = _ffn(x.reshape(b * seq, d), ffn2_pre_g[i], ffn2_w_gate[i], ffn2_w_up[i], ffn2_w_down[i],
                 ffn2_post_g[i]).reshape(b, seq, d)
    return x
```

```python
import functools

import jax
import jax.numpy as jnp
from jax import lax
from jax.experimental import pallas as pl
from jax.experimental.pallas import tpu as pltpu

F32 = jnp.float32
BF16 = jnp.bfloat16

EPS = 1e-6
LANES = 128
SUBLANES = 8
MXU_WIDTH = 256
VMEM_LIMIT_BYTES = 56 * 1024 * 1024

SSM_HEAD_DIM = 64
SSM_GROUPS = 2
SSM_STATE = 128
SSM_CONV_W = 4
CONF_KERNEL = 31

FFN_ROWS = 512
MIX_ROWS = 512
SSD_Q = 128
CONV_ROWS = 128
XBC_HALO = 8
U_HALO = 32


def _resident(shape):
    nd = len(shape)
    return pl.BlockSpec(shape, lambda *_: (0,) * nd, pipeline_mode=pl.Buffered(1))


def _rms(x, g):
    return x * lax.rsqrt(jnp.mean(x * x, axis=-1, keepdims=True) + EPS) * g


def _silu(x):
    return x * jax.nn.sigmoid(x)


def _dot(a, b):
    return jnp.dot(a, b, preferred_element_type=F32)


def _ffn_chunks(d_ff):
    cut = (d_ff // 2 + MXU_WIDTH - 1) // MXU_WIDTH * MXU_WIDTH
    return ((0, cut), (cut, d_ff)) if 0 < cut < d_ff else ((0, d_ff),)


def _ffn_kernel(x_ref, pre_g_ref, wg_ref, wu_ref, wd_ref, post_g_ref, o_ref):
    x = x_ref[...]
    xn = _rms(x, pre_g_ref[...]).astype(BF16)
    acc = jnp.zeros(x.shape, F32)
    for lo, hi in _ffn_chunks(wg_ref.shape[1]):
        g = _dot(xn, wg_ref[:, lo:hi])
        u = _dot(xn, wu_ref[:, lo:hi])
        acc = acc + _dot((_silu(g) * u).astype(BF16), wd_ref[lo:hi, :])
    o_ref[...] = x + 0.5 * _rms(acc, post_g_ref[...])


def _ffn(x2, pre_g, w_gate, w_up, w_down, post_g):
    m, d = x2.shape
    d_ff = w_gate.shape[1]
    row_spec = pl.BlockSpec((FFN_ROWS, d), lambda i: (i, 0))
    return pl.pallas_call(
        _ffn_kernel,
        grid=(m // FFN_ROWS,),
        in_specs=[row_spec, _resident((1, d)), _resident((d, d_ff)), _resident((d, d_ff)),
                  _resident((d_ff, d)), _resident((1, d))],
        out_specs=row_spec,
        out_shape=jax.ShapeDtypeStruct((m, d), F32),
        compiler_params=pltpu.CompilerParams(
            dimension_semantics=("arbitrary",), vmem_limit_bytes=VMEM_LIMIT_BYTES),
    )(x2, pre_g.reshape(1, d), w_gate.astype(BF16), w_up.astype(BF16), w_down.astype(BF16),
      post_g.reshape(1, d))


def _depthwise_conv(ext_ref, first_row, w_ref, b_ref, out_ref, n_rows):
    taps, channels = w_ref.shape
    halo = first_row + taps - 1
    win_rows = CONV_ROWS + halo
    by_shift = {}
    for k in range(taps):
        by_shift.setdefault((first_row + k) % SUBLANES, []).append((k, (first_row + k) // SUBLANES))
    for cb in range(channels // LANES):
        cs = slice(cb * LANES, (cb + 1) * LANES)
        for rb in range(n_rows // CONV_ROWS):
            window = ext_ref[rb * CONV_ROWS:rb * CONV_ROWS + win_rows, cs]
            acc = jnp.broadcast_to(b_ref[:, cs], (CONV_ROWS, LANES))
            for shift, group in sorted(by_shift.items()):
                rolled = window if shift == 0 else pltpu.roll(window, win_rows - shift, axis=0)
                for k, m in group:
                    acc = acc + w_ref[k:k + 1, cs] * rolled[SUBLANES * m:SUBLANES * m + CONV_ROWS, :]
            out_ref[rb * CONV_ROWS:(rb + 1) * CONV_ROWS, cs] = acc


def _mixer_kernel(x_ref, pre_g_ref, w_in_ref, b_glu_ref, cw_ref, cbias_ref, dtb_ref, alog_ref,
                  dskip_ref, ng_ref, dww_ref, dwb_ref, lng_ref, lnb_ref, expand_ref, w_out_ref,
                  post_g_ref, o_ref,
                  xbc_ext, xbc_conv, u_ext, u_conv, state, y_ssd,
                  *, d_ssm, d_conv, n_heads):
    tl = x_ref.shape[0]
    q = SSD_Q
    gn = SSM_GROUPS * SSM_STATE
    d_xbc = d_ssm + 2 * gn
    grp_w = d_ssm // SSM_GROUPS
    heads_per_group = n_heads // SSM_GROUPS

    @pl.when(pl.program_id(1) == 0)
    def _():
        xbc_ext[0:XBC_HALO, :] = jnp.zeros((XBC_HALO, d_xbc), F32)
        u_ext[0:U_HALO, :] = jnp.zeros((U_HALO, d_conv), F32)
        state[...] = jnp.zeros(state.shape, F32)

    x = x_ref[...]
    xn = _rms(x, pre_g_ref[...]).astype(BF16)

    o_xbc = d_ssm
    o_dt = o_xbc + d_xbc
    o_ga = o_dt + LANES
    o_gg = o_ga + d_conv
    z = _dot(xn, w_in_ref[:, 0:d_ssm])
    xbc_ext[XBC_HALO:XBC_HALO + tl, :] = _dot(xn, w_in_ref[:, o_xbc:o_dt])
    dt_raw = _dot(xn, w_in_ref[:, o_dt:o_ga])
    glu_a = _dot(xn, w_in_ref[:, o_ga:o_gg]) + b_glu_ref[:, 0:d_conv]
    glu_g = _dot(xn, w_in_ref[:, o_gg:o_gg + d_conv]) + b_glu_ref[:, d_conv:2 * d_conv]
    u_ext[U_HALO:U_HALO + tl, :] = glu_a * jax.nn.sigmoid(glu_g)

    _depthwise_conv(xbc_ext, XBC_HALO - (SSM_CONV_W - 1), cw_ref, cbias_ref, xbc_conv, tl)
    xbc_ext[0:XBC_HALO, :] = xbc_ext[tl:tl + XBC_HALO, :]
    _depthwise_conv(u_ext, U_HALO - (CONF_KERNEL - 1), dww_ref, dwb_ref, u_conv, tl)
    u_ext[0:U_HALO, :] = u_ext[tl:tl + U_HALO, :]

    lane_c = lax.broadcasted_iota(jnp.int32, (1, LANES), 1)
    head_lane = lane_c < n_heads
    a_neg = jnp.where(head_lane, -jnp.exp(alog_ref[...]), 0.0)
    dt_c = jnp.where(head_lane, jax.nn.softplus(dt_raw + dtb_ref[...]), 0.0)
    dt_r = dt_c.T
    acs_r = (dt_c * a_neg).T
    pos = lax.broadcasted_iota(jnp.int32, acs_r.shape, 1) % q
    shift = 1
    while shift < q:
        acs_r = acs_r + jnp.where(pos >= shift, pltpu.roll(acs_r, shift, axis=1), 0.0)
        shift *= 2
    acs_c = acs_r.T

    tril = (lax.broadcasted_iota(jnp.int32, (q, q), 0) >= lax.broadcasted_iota(jnp.int32, (q, q), 1))
    left = lax.broadcasted_iota(jnp.int32, (q, LANES), 1) < SSM_HEAD_DIM
    expand = expand_ref[...]

    def expand_heads(v):
        hi = v.astype(BF16)
        lo = (v - hi.astype(F32)).astype(BF16)
        return _dot(hi, expand) + _dot(lo, expand)

    for c in range(tl // q):
        rows = slice(c * q, (c + 1) * q)
        acs_k = acs_c[rows, :]
        acs_last = acs_k[q - 1:q, :]
        decay_in = expand_heads(jnp.exp(acs_k))
        w_end = expand_heads(dt_c[rows, :] * jnp.exp(acs_last - acs_k))
        xbc_k = _silu(xbc_conv[rows, :])
        xs = xbc_k[:, 0:d_ssm]
        xs_bf = xs.astype(BF16)
        xw_bf = (xs * w_end).astype(BF16)
        zk = z[rows, :]
        for g in range(SSM_GROUPS):
            gs = slice(g * grp_w, (g + 1) * grp_w)
            b_g = xbc_k[:, d_ssm + g * SSM_STATE:d_ssm + (g + 1) * SSM_STATE].astype(BF16)
            c_g = xbc_k[:, d_ssm + gn + g * SSM_STATE:d_ssm + gn + (g + 1) * SSM_STATE].astype(BF16)
            cb = lax.dot_general(c_g, b_g, (((1,), (1,)), ((), ())), preferred_element_type=F32)
            st_g = state[:, gs]
            y_g = _dot(c_g, st_g.astype(BF16)) * decay_in[:, gs]
            y_pairs = []
            for hp in range(heads_per_group // 2):
                ms = []
                for h in (g * heads_per_group + 2 * hp, g * heads_per_group + 2 * hp + 1):
                    seg = jnp.broadcast_to(acs_k[:, h:h + 1], (q, q)) - acs_r[h:h + 1, rows]
                    dec = jnp.exp(jnp.where(tril, seg, -1e30))
                    ms.append((cb * dec * dt_r[h:h + 1, rows]).astype(BF16))
                pair = xs_bf[:, g * grp_w + hp * LANES:g * grp_w + (hp + 1) * LANES]
                zero = jnp.zeros_like(pair)
                rhs = jnp.concatenate([jnp.where(left, pair, zero), jnp.where(left, zero, pair)], axis=0)
                y_pairs.append(_dot(jnp.concatenate(ms, axis=1), rhs))
            y_g = y_g + jnp.concatenate(y_pairs, axis=1) + dskip_ref[:, gs] * xs[:, gs]
            y_g = y_g * _silu(zk[:, gs])
            y_g = y_g * lax.rsqrt(jnp.mean(y_g * y_g, axis=-1, keepdims=True) + EPS)
            y_ssd[rows, gs] = (y_g * ng_ref[:, gs]).astype(BF16)
            upd = lax.dot_general(b_g, xw_bf[:, gs], (((0,), (0,)), ((), ())), preferred_element_type=F32)
            state[:, gs] = st_g * decay_in[q - 1:q, gs] + upd

    uc = u_conv[...]
    mu = jnp.mean(uc, axis=-1, keepdims=True)
    ucc = uc - mu
    var = jnp.mean(ucc * ucc, axis=-1, keepdims=True)
    y_conv = _silu(ucc * lax.rsqrt(var + EPS) * lng_ref[...] + lnb_ref[...]).astype(BF16)

    m = _dot(y_ssd[...], w_out_ref[0:d_ssm, :]) + _dot(y_conv, w_out_ref[d_ssm:d_ssm + d_conv, :])
    o_ref[...] = x + _rms(m, post_g_ref[...])


def _mixer(x, pre_g, w_in, b_glu, ssd_conv_w, ssd_conv_b, dt_bias, a_log, d_skip, ssd_norm_g,
           conf_dw_w, conf_dw_b, conf_ln_g, conf_ln_b, w_out, post_g):
    b, seq, d = x.shape
    n_heads = a_log.shape[0]
    d_ssm = n_heads * SSM_HEAD_DIM
    d_conv = conf_dw_w.shape[1]
    gn = SSM_GROUPS * SSM_STATE
    d_xbc = d_ssm + 2 * gn
    assert n_heads <= LANES and SSD_Q == LANES and MIX_ROWS % SSD_Q == 0 and seq % MIX_ROWS == 0

    def row(v):
        return v.reshape(1, -1)

    def pad_lanes(v):
        return jnp.pad(v, (0, LANES - v.shape[0])).reshape(1, LANES)

    o_dt = d_ssm + d_xbc
    w_in_k = jnp.concatenate(
        [w_in[:, :o_dt], jnp.pad(w_in[:, o_dt:o_dt + n_heads], ((0, 0), (0, LANES - n_heads))),
         w_in[:, o_dt + n_heads:]], axis=1).astype(BF16)
    d_in_k = w_in_k.shape[1]
    expand = (jnp.arange(LANES)[:, None] == (jnp.arange(d_ssm) // SSM_HEAD_DIM)[None, :]).astype(BF16)
    dskip_x = jnp.repeat(d_skip, SSM_HEAD_DIM).reshape(1, d_ssm)

    tile = pl.BlockSpec((None, MIX_ROWS, d), lambda bi, ti: (bi, ti, 0))
    kern = functools.partial(_mixer_kernel, d_ssm=d_ssm, d_conv=d_conv, n_heads=n_heads)
    return pl.pallas_call(
        kern,
        grid=(b, seq // MIX_ROWS),
        in_specs=[tile, _resident((1, d)), _resident((d, d_in_k)), _resident((1, 2 * d_conv)),
                  _resident((SSM_CONV_W, d_xbc)), _resident((1, d_xbc)), _resident((1, LANES)),
                  _resident((1, LANES)), _resident((1, d_ssm)), _resident((1, d_ssm)),
                  _resident((CONF_KERNEL, d_conv)), _resident((1, d_conv)), _resident((1, d_conv)),
                  _resident((1, d_conv)), _resident((LANES, d_ssm)), _resident((d_ssm + d_conv, d)),
                  _resident((1, d))],
        out_specs=tile,
        out_shape=jax.ShapeDtypeStruct((b, seq, d), F32),
        scratch_shapes=[
            pltpu.VMEM((XBC_HALO + MIX_ROWS, d_xbc), F32),
            pltpu.VMEM((MIX_ROWS, d_xbc), F32),
            pltpu.VMEM((U_HALO + MIX_ROWS, d_conv), F32),
            pltpu.VMEM((MIX_ROWS, d_conv), F32),
            pltpu.VMEM((SSM_STATE, d_ssm), F32),
            pltpu.VMEM((MIX_ROWS, d_ssm), BF16),
        ],
        compiler_params=pltpu.CompilerParams(
            dimension_semantics=("arbitrary", "arbitrary"), vmem_limit_bytes=VMEM_LIMIT_BYTES),
    )(x, row(pre_g), w_in_k, row(b_glu), ssd_conv_w, row(ssd_conv_b), pad_lanes(dt_bias),
      pad_lanes(a_log), dskip_x, row(ssd_norm_g), conf_dw_w, row(conf_dw_b), row(conf_ln_g),
      row(conf_ln_b), expand, w_out.astype(BF16), row(post_g))


def kernel(x, ffn1_pre_g, ffn1_w_gate, ffn1_w_up, ffn1_w_down, ffn1_post_g, mix_pre_g, w_in, b_glu, ssd_conv_w, ssd_conv_b, dt_bias, a_log, d_skip, ssd_norm_g, conf_dw_w, conf_dw_b, conf_ln_g, conf_ln_b, w_out, mix_post_g, ffn2_pre_g, ffn2_w_gate, ffn2_w_up, ffn2_w_down, ffn2_post_g):
    b, seq, d = x.shape
    for i in range(ffn1_pre_g.shape[0]):
        x = _ffn(x.reshape(b * seq, d), ffn1_pre_g[i], ffn1_w_gate[i], ffn1_w_up[i], ffn1_w_down[i],
                 ffn1_post_g[i]).reshape(b, seq, d)
        x = _mixer(x, mix_pre_g[i], w_in[i], b_glu[i], ssd_conv_w[i], ssd_conv_b[i], dt_bias[i],
                   a_log[i], d_skip[i], ssd_norm_g[i], conf_dw_w[i], conf_dw_b[i], conf_ln_g[i],
                   conf_ln_b[i], w_out[i], mix_post_g[i])
        x = _ffn(x.reshape(b * seq, d), ffn2_pre_g[i], ffn2_w_gate[i], ffn2_w_up[i], ffn2_w_down[i],
                 ffn2_post_g[i]).reshape(b, seq, d)
    return x
```

```python
import functools

import jax
import jax.numpy as jnp
from jax import lax
from jax.experimental import pallas as pl
from jax.experimental.pallas import tpu as pltpu

F32 = jnp.float32
BF16 = jnp.bfloat16

EPS = 1e-6
LANES = 128
SUBLANES = 8
MXU_WIDTH = 256
VMEM_LIMIT_BYTES = 56 * 1024 * 1024

SSM_HEAD_DIM = 64
SSM_GROUPS = 2
SSM_STATE = 128
SSM_CONV_W = 4
CONF_KERNEL = 31

FFN_ROWS = 512
MIX_ROWS = 512
SSD_Q = 128
CONV_ROWS = 128
XBC_HALO = 8
U_HALO = 32
FFN_COLS = MXU_WIDTH
GLU_COLS = MXU_WIDTH


def _resident(shape):
    nd = len(shape)
    return pl.BlockSpec(shape, lambda *_: (0,) * nd, pipeline_mode=pl.Buffered(1))


def _rms(x, g):
    return x * lax.rsqrt(jnp.mean(x * x, axis=-1, keepdims=True) + EPS) * g


def _silu(x):
    return x * jax.nn.sigmoid(x)


def _dot(a, b):
    return jnp.dot(a, b, preferred_element_type=F32)


def _conv_block(window, first_row, w_ref, b_ref, cs):
    taps = w_ref.shape[0]
    win_rows = window.shape[0]
    assert win_rows == CONV_ROWS + first_row + taps - 1 and win_rows % SUBLANES == 0
    by_shift = {}
    for k in range(taps):
        by_shift.setdefault((first_row + k) % SUBLANES, []).append((k, (first_row + k) // SUBLANES))
    acc = jnp.broadcast_to(b_ref[:, cs], (CONV_ROWS, LANES))
    for shift, group in sorted(by_shift.items()):
        rolled = window if shift == 0 else pltpu.roll(window, win_rows - shift, axis=0)
        for k, m in group:
            acc = acc + w_ref[k:k + 1, cs] * rolled[SUBLANES * m:SUBLANES * m + CONV_ROWS, :]
    return acc


def _ffn_chunks(d_ff):
    cut = (d_ff // 2 + MXU_WIDTH - 1) // MXU_WIDTH * MXU_WIDTH
    return ((0, cut), (cut, d_ff)) if 0 < cut < d_ff else ((0, d_ff),)


def _ffn_kernel(x_ref, pre_g_ref, wg_ref, wu_ref, wd_ref, post_g_ref, o_ref):
    x = x_ref[...]
    xn = _rms(x, pre_g_ref[...]).astype(BF16)
    acc = jnp.zeros(x.shape, F32)
    for lo, hi in _ffn_chunks(wg_ref.shape[1]):
        g = _dot(xn, wg_ref[:, lo:hi])
        u = _dot(xn, wu_ref[:, lo:hi])
        acc = acc + _dot((_silu(g) * u).astype(BF16), wd_ref[lo:hi, :])
    o_ref[...] = x + 0.5 * _rms(acc, post_g_ref[...])


def _ffn(x2, pre_g, w_gate, w_up, w_down, post_g):
    m, d = x2.shape
    d_ff = w_gate.shape[1]
    row_spec = pl.BlockSpec((FFN_ROWS, d), lambda i: (i, 0))
    return pl.pallas_call(
        _ffn_kernel,
        grid=(m // FFN_ROWS,),
        in_specs=[row_spec, _resident((1, d)), _resident((d, d_ff)), _resident((d, d_ff)),
                  _resident((d_ff, d)), _resident((1, d))],
        out_specs=row_spec,
        out_shape=jax.ShapeDtypeStruct((m, d), F32),
        compiler_params=pltpu.CompilerParams(
            dimension_semantics=("arbitrary",), vmem_limit_bytes=VMEM_LIMIT_BYTES),
    )(x2, pre_g.reshape(1, d), w_gate.astype(BF16), w_up.astype(BF16), w_down.astype(BF16),
      post_g.reshape(1, d))


def _ffn_conv_kernel(x_ref, pre_g_ref, wg_ref, wu_ref, wd_ref, post_g_ref, mix_g_ref, wglu_ref,
                     bglu_ref, dww_ref, dwb_ref, lng_ref, lnb_ref,
                     x1_ref, yconv_ref,
                     xn_ref, acc_ref, u_ring, uconv_ref, *, tiles_per_seq):
    s = pl.program_id(0)
    cur = s % 2
    prev = 1 - cur
    rows = x_ref.shape[0]
    d_ff = wg_ref.shape[1]
    d_conv = dww_ref.shape[1]
    n_rb = rows // CONV_ROWS
    n_blocks = (d_conv // LANES) * n_rb
    first_row = U_HALO - (CONF_KERNEL - 1)
    n_ffn_iter = d_ff // FFN_COLS
    blocks_per_iter = -(-n_blocks * FFN_COLS // d_ff)
    loop_blocks = min(n_blocks, n_ffn_iter * blocks_per_iter)
    n_glu_iter = d_conv // GLU_COLS
    norm_rows = rows // n_glu_iter
    assert rows % CONV_ROWS == 0 and d_conv % GLU_COLS == 0 and norm_rows % (2 * SUBLANES) == 0

    @pl.when(s == 0)
    def _():
        u_ring[...] = jnp.zeros(u_ring.shape, F32)

    x = x_ref[...]
    xn_ref[...] = _rms(x, pre_g_ref[...]).astype(BF16)
    acc_ref[...] = jnp.zeros(acc_ref.shape, F32)

    def ffn_cols(sl):
        xn = xn_ref[...]
        g = _dot(xn, wg_ref[:, sl])
        u = _dot(xn, wu_ref[:, sl])
        return _dot((_silu(g) * u).astype(BF16), wd_ref[sl, :])

    def conv_block(blk):
        cs = pl.ds(pl.multiple_of(lax.div(blk, n_rb) * LANES, LANES), LANES)
        r0 = pl.multiple_of(lax.rem(blk, n_rb) * CONV_ROWS, CONV_ROWS)
        window = u_ring[prev, pl.ds(r0, CONV_ROWS + U_HALO), cs]
        uconv_ref[pl.ds(r0, CONV_ROWS), cs] = _conv_block(window, first_row, dww_ref, dwb_ref, cs)

    def ffn_and_conv(i, carry):
        acc_ref[...] += ffn_cols(pl.ds(pl.multiple_of(i * FFN_COLS, FFN_COLS), FFN_COLS))
        for j in range(blocks_per_iter):
            conv_block(jnp.minimum(i * blocks_per_iter + j, loop_blocks - 1))
        return carry

    lax.fori_loop(0, n_ffn_iter, ffn_and_conv, 0)
    acc = acc_ref[...]
    if n_ffn_iter * FFN_COLS < d_ff:
        acc = acc + ffn_cols(slice(n_ffn_iter * FFN_COLS, d_ff))
    for blk in range(loop_blocks, n_blocks):
        conv_block(jnp.int32(blk))

    x1 = x + 0.5 * _rms(acc, post_g_ref[...])
    x1_ref[...] = x1
    xn_ref[...] = _rms(x1, mix_g_ref[...]).astype(BF16)

    tail = u_ring[prev, rows:rows + U_HALO, :]
    u_ring[cur, 0:U_HALO, :] = jnp.where(s % tiles_per_seq == 0, jnp.zeros_like(tail), tail)

    def glu_and_norm(j, carry):
        ca = pl.ds(pl.multiple_of(j * GLU_COLS, GLU_COLS), GLU_COLS)
        cg = pl.ds(pl.multiple_of(d_conv + j * GLU_COLS, GLU_COLS), GLU_COLS)
        xm = xn_ref[...]
        glu_a = _dot(xm, wglu_ref[:, ca]) + bglu_ref[:, ca]
        glu_g = _dot(xm, wglu_ref[:, cg]) + bglu_ref[:, cg]
        u_ring[cur, U_HALO:U_HALO + rows, ca] = glu_a * jax.nn.sigmoid(glu_g)
        rs = pl.ds(pl.multiple_of(j * norm_rows, norm_rows), norm_rows)
        uc = uconv_ref[rs, :]
        mu = jnp.mean(uc, axis=-1, keepdims=True)
        ucc = uc - mu
        var = jnp.mean(ucc * ucc, axis=-1, keepdims=True)
        yconv_ref[rs, :] = _silu(ucc * lax.rsqrt(var + EPS) * lng_ref[...] + lnb_ref[...]).astype(BF16)
        return carry

    lax.fori_loop(0, n_glu_iter, glu_and_norm, 0)


def _ffn_conv(x2, seq, pre_g, w_gate, w_up, w_down, post_g, mix_pre_g, w_glu, b_glu, conf_dw_w,
              conf_dw_b, conf_ln_g, conf_ln_b):
    m, d = x2.shape
    d_ff = w_gate.shape[1]
    d_conv = conf_dw_w.shape[1]
    n_tiles = m // FFN_ROWS
    assert seq % FFN_ROWS == 0 and d_ff % MXU_WIDTH == 0 and conf_dw_w.shape[0] == CONF_KERNEL

    def row(v):
        return v.reshape(1, -1)

    this_tile = lambda s: (jnp.minimum(s, n_tiles - 1), 0)
    prev_tile = lambda s: (jnp.maximum(s - 1, 0), 0)
    kern = functools.partial(_ffn_conv_kernel, tiles_per_seq=seq // FFN_ROWS)
    return pl.pallas_call(
        kern,
        grid=(n_tiles + 1,),
        in_specs=[pl.BlockSpec((FFN_ROWS, d), this_tile), _resident((1, d)), _resident((d, d_ff)),
                  _resident((d, d_ff)), _resident((d_ff, d)), _resident((1, d)), _resident((1, d)),
                  _resident((d, 2 * d_conv)), _resident((1, 2 * d_conv)),
                  _resident((CONF_KERNEL, d_conv)), _resident((1, d_conv)), _resident((1, d_conv)),
                  _resident((1, d_conv))],
        out_specs=[pl.BlockSpec((FFN_ROWS, d), this_tile), pl.BlockSpec((FFN_ROWS, d_conv), prev_tile)],
        out_shape=[jax.ShapeDtypeStruct((m, d), F32), jax.ShapeDtypeStruct((m, d_conv), BF16)],
        scratch_shapes=[
            pltpu.VMEM((FFN_ROWS, d), BF16),
            pltpu.VMEM((FFN_ROWS, d), F32),
            pltpu.VMEM((2, U_HALO + FFN_ROWS, d_conv), F32),
            pltpu.VMEM((FFN_ROWS, d_conv), F32),
        ],
        compiler_params=pltpu.CompilerParams(
            dimension_semantics=("arbitrary",), vmem_limit_bytes=VMEM_LIMIT_BYTES),
    )(x2, row(pre_g), w_gate.astype(BF16), w_up.astype(BF16), w_down.astype(BF16), row(post_g),
      row(mix_pre_g), w_glu.astype(BF16), row(b_glu), conf_dw_w, row(conf_dw_b), row(conf_ln_g),
      row(conf_ln_b))


def _mixer_kernel(x_ref, yconv_ref, pre_g_ref, w_in_ref, cw_ref, cbias_ref, dtb_ref, alog_ref,
                  dskip_ref, ng_ref, expand_ref, w_out_ref, post_g_ref, o_ref,
                  xbc_ext, xbc_conv, state, y_ssd,
                  *, d_ssm, n_heads):
    tl = x_ref.shape[0]
    q = SSD_Q
    gn = SSM_GROUPS * SSM_STATE
    d_xbc = d_ssm + 2 * gn
    d_conv = yconv_ref.shape[1]
    grp_w = d_ssm // SSM_GROUPS
    heads_per_group = n_heads // SSM_GROUPS

    @pl.when(pl.program_id(1) == 0)
    def _():
        xbc_ext[0:XBC_HALO, :] = jnp.zeros((XBC_HALO, d_xbc), F32)
        state[...] = jnp.zeros(state.shape, F32)

    x = x_ref[...]
    xn = _rms(x, pre_g_ref[...]).astype(BF16)

    o_dt = d_ssm + d_xbc
    z = _dot(xn, w_in_ref[:, 0:d_ssm])
    xbc_ext[XBC_HALO:XBC_HALO + tl, :] = _dot(xn, w_in_ref[:, d_ssm:o_dt])
    dt_raw = _dot(xn, w_in_ref[:, o_dt:o_dt + LANES])

    for cb in range(d_xbc // LANES):
        cs = slice(cb * LANES, (cb + 1) * LANES)
        for rb in range(tl // CONV_ROWS):
            window = xbc_ext[rb * CONV_ROWS:(rb + 1) * CONV_ROWS + XBC_HALO, cs]
            xbc_conv[rb * CONV_ROWS:(rb + 1) * CONV_ROWS, cs] = _conv_block(
                window, XBC_HALO - (SSM_CONV_W - 1), cw_ref, cbias_ref, cs)
    xbc_ext[0:XBC_HALO, :] = xbc_ext[tl:tl + XBC_HALO, :]

    lane_c = lax.broadcasted_iota(jnp.int32, (1, LANES), 1)
    head_lane = lane_c < n_heads
    a_neg = jnp.where(head_lane, -jnp.exp(alog_ref[...]), 0.0)
    dt_c = jnp.where(head_lane, jax.nn.softplus(dt_raw + dtb_ref[...]), 0.0)
    dt_r = dt_c.T
    acs_r = (dt_c * a_neg).T
    pos = lax.broadcasted_iota(jnp.int32, acs_r.shape, 1) % q
    shift = 1
    while shift < q:
        acs_r = acs_r + jnp.where(pos >= shift, pltpu.roll(acs_r, shift, axis=1), 0.0)
        shift *= 2
    acs_c = acs_r.T

    tril = (lax.broadcasted_iota(jnp.int32, (q, q), 0) >= lax.broadcasted_iota(jnp.int32, (q, q), 1))
    left = lax.broadcasted_iota(jnp.int32, (q, LANES), 1) < SSM_HEAD_DIM
    expand = expand_ref[...]

    def expand_heads(v):
        hi = v.astype(BF16)
        lo = (v - hi.astype(F32)).astype(BF16)
        return _dot(hi, expand) + _dot(lo, expand)

    for c in range(tl // q):
        rows = slice(c * q, (c + 1) * q)
        acs_k = acs_c[rows, :]
        acs_last = acs_k[q - 1:q, :]
        decay_in = expand_heads(jnp.exp(acs_k))
        w_end = expand_heads(dt_c[rows, :] * jnp.exp(acs_last - acs_k))
        xbc_k = _silu(xbc_conv[rows, :])
        xs = xbc_k[:, 0:d_ssm]
        xs_bf = xs.astype(BF16)
        xw_bf = (xs * w_end).astype(BF16)
        zk = z[rows, :]
        for g in range(SSM_GROUPS):
            gs = slice(g * grp_w, (g + 1) * grp_w)
            b_g = xbc_k[:, d_ssm + g * SSM_STATE:d_ssm + (g + 1) * SSM_STATE].astype(BF16)
            c_g = xbc_k[:, d_ssm + gn + g * SSM_STATE:d_ssm + gn + (g + 1) * SSM_STATE].astype(BF16)
            cb = lax.dot_general(c_g, b_g, (((1,), (1,)), ((), ())), preferred_element_type=F32)
            st_g = state[:, gs]
            y_g = _dot(c_g, st_g.astype(BF16)) * decay_in[:, gs]
            y_pairs = []
            for hp in range(heads_per_group // 2):
                ms = []
                for h in (g * heads_per_group + 2 * hp, g * heads_per_group + 2 * hp + 1):
                    seg = jnp.broadcast_to(acs_k[:, h:h + 1], (q, q)) - acs_r[h:h + 1, rows]
                    dec = jnp.exp(jnp.where(tril, seg, -1e30))
                    ms.append((cb * dec * dt_r[h:h + 1, rows]).astype(BF16))
                pair = xs_bf[:, g * grp_w + hp * LANES:g * grp_w + (hp + 1) * LANES]
                zero = jnp.zeros_like(pair)
                rhs = jnp.concatenate([jnp.where(left, pair, zero), jnp.where(left, zero, pair)], axis=0)
                y_pairs.append(_dot(jnp.concatenate(ms, axis=1), rhs))
            y_g = y_g + jnp.concatenate(y_pairs, axis=1) + dskip_ref[:, gs] * xs[:, gs]
            y_g = y_g * _silu(zk[:, gs])
            y_g = y_g * lax.rsqrt(jnp.mean(y_g * y_g, axis=-1, keepdims=True) + EPS)
            y_ssd[rows, gs] = (y_g * ng_ref[:, gs]).astype(BF16)
            upd = lax.dot_general(b_g, xw_bf[:, gs], (((0,), (0,)), ((), ())), preferred_element_type=F32)
            state[:, gs] = st_g * decay_in[q - 1:q, gs] + upd

    m = _dot(y_ssd[...], w_out_ref[0:d_ssm, :]) + _dot(yconv_ref[...], w_out_ref[d_ssm:d_ssm + d_conv, :])
    o_ref[...] = x + _rms(m, post_g_ref[...])


def _mixer(x, y_conv, pre_g, w_in_ssd, ssd_conv_w, ssd_conv_b, dt_bias, a_log, d_skip, ssd_norm_g,
           w_out, post_g):
    b, seq, d = x.shape
    n_heads = a_log.shape[0]
    d_ssm = n_heads * SSM_HEAD_DIM
    d_conv = y_conv.shape[-1]
    gn = SSM_GROUPS * SSM_STATE
    d_xbc = d_ssm + 2 * gn
    assert n_heads <= LANES and SSD_Q == LANES and MIX_ROWS % SSD_Q == 0 and seq % MIX_ROWS == 0
    assert ssd_conv_w.shape[0] == SSM_CONV_W

    def row(v):
        return v.reshape(1, -1)

    def pad_lanes(v):
        return jnp.pad(v, (0, LANES - v.shape[0])).reshape(1, LANES)

    o_dt = d_ssm + d_xbc
    w_in_k = jnp.pad(w_in_ssd, ((0, 0), (0, LANES - n_heads))).astype(BF16)
    d_in_k = w_in_k.shape[1]
    assert d_in_k == o_dt + LANES
    expand = (jnp.arange(LANES)[:, None] == (jnp.arange(d_ssm) // SSM_HEAD_DIM)[None, :]).astype(BF16)
    dskip_x = jnp.repeat(d_skip, SSM_HEAD_DIM).reshape(1, d_ssm)

    tile = pl.BlockSpec((None, MIX_ROWS, d), lambda bi, ti: (bi, ti, 0))
    conv_tile = pl.BlockSpec((None, MIX_ROWS, d_conv), lambda bi, ti: (bi, ti, 0))
    kern = functools.partial(_mixer_kernel, d_ssm=d_ssm, n_heads=n_heads)
    return pl.pallas_call(
        kern,
        grid=(b, seq // MIX_ROWS),
        in_specs=[tile, conv_tile, _resident((1, d)), _resident((d, d_in_k)),
                  _resident((SSM_CONV_W, d_xbc)), _resident((1, d_xbc)), _resident((1, LANES)),
                  _resident((1, LANES)), _resident((1, d_ssm)), _resident((1, d_ssm)),
                  _resident((LANES, d_ssm)), _resident((d_ssm + d_conv, d)), _resident((1, d))],
        out_specs=tile,
        out_shape=jax.ShapeDtypeStruct((b, seq, d), F32),
        scratch_shapes=[
            pltpu.VMEM((XBC_HALO + MIX_ROWS, d_xbc), F32),
            pltpu.VMEM((MIX_ROWS, d_xbc), F32),
            pltpu.VMEM((SSM_STATE, d_ssm), F32),
            pltpu.VMEM((MIX_ROWS, d_ssm), BF16),
        ],
        compiler_params=pltpu.CompilerParams(
            dimension_semantics=("arbitrary", "arbitrary"), vmem_limit_bytes=VMEM_LIMIT_BYTES),
    )(x, y_conv, row(pre_g), w_in_k, ssd_conv_w, row(ssd_conv_b), pad_lanes(dt_bias),
      pad_lanes(a_log), dskip_x, row(ssd_norm_g), expand, w_out.astype(BF16), row(post_g))


def kernel(x, ffn1_pre_g, ffn1_w_gate, ffn1_w_up, ffn1_w_down, ffn1_post_g, mix_pre_g, w_in, b_glu, ssd_conv_w, ssd_conv_b, dt_bias, a_log, d_skip, ssd_norm_g, conf_dw_w, conf_dw_b, conf_ln_g, conf_ln_b, w_out, mix_post_g, ffn2_pre_g, ffn2_w_gate, ffn2_w_up, ffn2_w_down, ffn2_post_g):
    b, seq, d = x.shape
    for i in range(ffn1_pre_g.shape[0]):
        d_conv = conf_dw_w.shape[-1]
        n_ssd_cols = w_in.shape[-1] - 2 * d_conv
        x1, y_conv = _ffn_conv(
            x.reshape(b * seq, d), seq, ffn1_pre_g[i], ffn1_w_gate[i], ffn1_w_up[i], ffn1_w_down[i],
            ffn1_post_g[i], mix_pre_g[i], w_in[i][:, n_ssd_cols:], b_glu[i], conf_dw_w[i],
            conf_dw_b[i], conf_ln_g[i], conf_ln_b[i])
        x = _mixer(x1.reshape(b, seq, d), y_conv.reshape(b, seq, d_conv), mix_pre_g[i],
                   w_in[i][:, :n_ssd_cols], ssd_conv_w[i], ssd_conv_b[i], dt_bias[i], a_log[i],
                   d_skip[i], ssd_norm_g[i], w_out[i], mix_post_g[i])
        x = _ffn(x.reshape(b * seq, d), ffn2_pre_g[i], ffn2_w_gate[i], ffn2_w_up[i], ffn2_w_down[i],
                 ffn2_post_g[i]).reshape(b, seq, d)
    return x
```

```python
import functools

import jax
import jax.numpy as jnp
from jax import lax
from jax.experimental import pallas as pl
from jax.experimental.pallas import tpu as pltpu

F32 = jnp.float32
BF16 = jnp.bfloat16

EPS = 1e-6
LANES = 128
SUBLANES = 8
MXU_WIDTH = 256
VMEM_LIMIT_BYTES = 56 * 1024 * 1024

SSM_HEAD_DIM = 64
SSM_GROUPS = 2
SSM_STATE = 128
SSM_CONV_W = 4
CONF_KERNEL = 31

FFN_ROWS = 512
MIX_ROWS = 512
SSD_Q = 128
CONV_ROWS = 128
XBC_HALO = 8
U_HALO = 32
W_STAGE_CHUNKS = 4


def _resident(shape):
    nd = len(shape)
    return pl.BlockSpec(shape, lambda *_: (0,) * nd, pipeline_mode=pl.Buffered(1))


def _rms(x, g):
    return x * lax.rsqrt(jnp.mean(x * x, axis=-1, keepdims=True) + EPS) * g


def _silu(x):
    return x * jax.nn.sigmoid(x)


def _dot(a, b):
    return jnp.dot(a, b, preferred_element_type=F32)


def _ffn_chunks(d_ff):
    cut = (d_ff // 2 + MXU_WIDTH - 1) // MXU_WIDTH * MXU_WIDTH
    return ((0, cut), (cut, d_ff)) if 0 < cut < d_ff else ((0, d_ff),)


def _stream_cast(src_hbm, dst_ref, stage_ref, sem_ref):
    rows = stage_ref.shape[1]
    n_chunks = src_hbm.shape[0] // rows
    assert n_chunks * rows == src_hbm.shape[0] and src_hbm.shape[1:] == stage_ref.shape[2:]

    def chunk_copy(c):
        return pltpu.make_async_copy(src_hbm.at[pl.ds(c * rows, rows)], stage_ref.at[c % 2],
                                     sem_ref.at[c % 2])

    chunk_copy(0).start()
    for c in range(n_chunks):
        if c + 1 < n_chunks:
            chunk_copy(c + 1).start()
        chunk_copy(c).wait()
        dst_ref[c * rows:(c + 1) * rows, :] = stage_ref[c % 2].astype(BF16)


def _ffn_kernel(x_ref, pre_g_ref, wg_hbm, wu_hbm, wd_hbm, post_g_ref, o_ref,
                wg_ref, wu_ref, wd_ref, stage_in, stage_out, sem):
    @pl.when(pl.program_id(0) == 0)
    def _():
        _stream_cast(wg_hbm, wg_ref, stage_in, sem)
        _stream_cast(wu_hbm, wu_ref, stage_in, sem)
        _stream_cast(wd_hbm, wd_ref, stage_out, sem)

    x = x_ref[...]
    xn = _rms(x, pre_g_ref[...]).astype(BF16)
    acc = jnp.zeros(x.shape, F32)
    for lo, hi in _ffn_chunks(wg_ref.shape[1]):
        g = _dot(xn, wg_ref[:, lo:hi])
        u = _dot(xn, wu_ref[:, lo:hi])
        acc = acc + _dot((_silu(g) * u).astype(BF16), wd_ref[lo:hi, :])
    o_ref[...] = x + 0.5 * _rms(acc, post_g_ref[...])


def _ffn(x2, pre_g, w_gate, w_up, w_down, post_g):
    m, d = x2.shape
    d_ff = w_gate.shape[1]
    assert d % W_STAGE_CHUNKS == 0 and d_ff % W_STAGE_CHUNKS == 0
    row_spec = pl.BlockSpec((FFN_ROWS, d), lambda i: (i, 0))
    in_hbm = pl.BlockSpec(memory_space=pl.ANY)
    return pl.pallas_call(
        _ffn_kernel,
        grid=(m // FFN_ROWS,),
        in_specs=[row_spec, _resident((1, d)), in_hbm, in_hbm, in_hbm, _resident((1, d))],
        out_specs=row_spec,
        out_shape=jax.ShapeDtypeStruct((m, d), F32),
        scratch_shapes=[
            pltpu.VMEM((d, d_ff), BF16), pltpu.VMEM((d, d_ff), BF16), pltpu.VMEM((d_ff, d), BF16),
            pltpu.VMEM((2, d // W_STAGE_CHUNKS, d_ff), F32),
            pltpu.VMEM((2, d_ff // W_STAGE_CHUNKS, d), F32),
            pltpu.SemaphoreType.DMA((2,)),
        ],
        compiler_params=pltpu.CompilerParams(
            dimension_semantics=("arbitrary",), vmem_limit_bytes=VMEM_LIMIT_BYTES),
    )(x2, pre_g.reshape(1, d), w_gate, w_up, w_down, post_g.reshape(1, d))


def _depthwise_conv(ext_ref, first_row, w_ref, b_ref, out_ref, n_rows):
    taps, channels = w_ref.shape
    halo = first_row + taps - 1
    win_rows = CONV_ROWS + halo
    by_shift = {}
    for k in range(taps):
        by_shift.setdefault((first_row + k) % SUBLANES, []).append((k, (first_row + k) // SUBLANES))
    for cb in range(channels // LANES):
        cs = slice(cb * LANES, (cb + 1) * LANES)
        for rb in range(n_rows // CONV_ROWS):
            window = ext_ref[rb * CONV_ROWS:rb * CONV_ROWS + win_rows, cs]
            acc = jnp.broadcast_to(b_ref[:, cs], (CONV_ROWS, LANES))
            for shift, group in sorted(by_shift.items()):
                rolled = window if shift == 0 else pltpu.roll(window, win_rows - shift, axis=0)
                for k, m in group:
                    acc = acc + w_ref[k:k + 1, cs] * rolled[SUBLANES * m:SUBLANES * m + CONV_ROWS, :]
            out_ref[rb * CONV_ROWS:(rb + 1) * CONV_ROWS, cs] = acc


def _mixer_kernel(x_ref, pre_g_ref, w_in_ref, b_glu_ref, cw_ref, cbias_ref, dtb_ref, alog_ref,
                  dskip_ref, ng_ref, dww_ref, dwb_ref, lng_ref, lnb_ref, expand_ref, w_out_ref,
                  post_g_ref, o_ref,
                  xbc_ext, xbc_conv, u_ext, u_conv, state, y_mix,
                  *, d_ssm, d_conv, n_heads):
    tl = x_ref.shape[0]
    q = SSD_Q
    gn = SSM_GROUPS * SSM_STATE
    d_xbc = d_ssm + 2 * gn
    grp_w = d_ssm // SSM_GROUPS
    heads_per_group = n_heads // SSM_GROUPS

    @pl.when(pl.program_id(1) == 0)
    def _():
        xbc_ext[0:XBC_HALO, :] = jnp.zeros((XBC_HALO, d_xbc), F32)
        u_ext[0:U_HALO, :] = jnp.zeros((U_HALO, d_conv), F32)
        state[...] = jnp.zeros(state.shape, F32)

    x = x_ref[...]
    xn = _rms(x, pre_g_ref[...]).astype(BF16)

    o_xbc = d_ssm
    o_dt = o_xbc + d_xbc
    o_ga = o_dt + LANES
    o_gg = o_ga + d_conv
    z = _dot(xn, w_in_ref[:, 0:d_ssm])
    xbc_ext[XBC_HALO:XBC_HALO + tl, :] = _dot(xn, w_in_ref[:, o_xbc:o_dt])
    dt_raw = _dot(xn, w_in_ref[:, o_dt:o_ga])
    glu_a = _dot(xn, w_in_ref[:, o_ga:o_gg]) + b_glu_ref[:, 0:d_conv]
    glu_g = _dot(xn, w_in_ref[:, o_gg:o_gg + d_conv]) + b_glu_ref[:, d_conv:2 * d_conv]
    u_ext[U_HALO:U_HALO + tl, :] = glu_a * jax.nn.sigmoid(glu_g)

    _depthwise_conv(xbc_ext, XBC_HALO - (SSM_CONV_W - 1), cw_ref, cbias_ref, xbc_conv, tl)
    xbc_ext[0:XBC_HALO, :] = xbc_ext[tl:tl + XBC_HALO, :]
    _depthwise_conv(u_ext, U_HALO - (CONF_KERNEL - 1), dww_ref, dwb_ref, u_conv, tl)
    u_ext[0:U_HALO, :] = u_ext[tl:tl + U_HALO, :]

    lane_c = lax.broadcasted_iota(jnp.int32, (1, LANES), 1)
    head_lane = lane_c < n_heads
    a_neg = jnp.where(head_lane, -jnp.exp(alog_ref[...]), 0.0)
    dt_c = jnp.where(head_lane, jax.nn.softplus(dt_raw + dtb_ref[...]), 0.0)
    dt_r = dt_c.T
    acs_r = (dt_c * a_neg).T
    pos = lax.broadcasted_iota(jnp.int32, acs_r.shape, 1) % q
    shift = 1
    while shift < q:
        acs_r = acs_r + jnp.where(pos >= shift, pltpu.roll(acs_r, shift, axis=1), 0.0)
        shift *= 2
    acs_c = acs_r.T

    tril = (lax.broadcasted_iota(jnp.int32, (q, q), 0) >= lax.broadcasted_iota(jnp.int32, (q, q), 1))
    left = lax.broadcasted_iota(jnp.int32, (q, LANES), 1) < SSM_HEAD_DIM
    expand = expand_ref[...]

    def expand_heads(v):
        hi = v.astype(BF16)
        lo = (v - hi.astype(F32)).astype(BF16)
        return _dot(jnp.concatenate([hi, lo], axis=1), expand)

    for c in range(tl // q):
        rows = slice(c * q, (c + 1) * q)
        acs_k = acs_c[rows, :]
        acs_last = acs_k[q - 1:q, :]
        decay_in = expand_heads(jnp.exp(acs_k))
        w_end = expand_heads(dt_c[rows, :] * jnp.exp(acs_last - acs_k))
        xbc_k = _silu(xbc_conv[rows, :])
        xs = xbc_k[:, 0:d_ssm]
        xs_bf = xs.astype(BF16)
        xw_bf = (xs * w_end).astype(BF16)
        zk = z[rows, :]
        for g in range(SSM_GROUPS):
            gs = slice(g * grp_w, (g + 1) * grp_w)
            b_g = xbc_k[:, d_ssm + g * SSM_STATE:d_ssm + (g + 1) * SSM_STATE].astype(BF16)
            c_g = xbc_k[:, d_ssm + gn + g * SSM_STATE:d_ssm + gn + (g + 1) * SSM_STATE].astype(BF16)
            cb = lax.dot_general(c_g, b_g, (((1,), (1,)), ((), ())), preferred_element_type=F32)
            st_g = state[:, gs]
            y_g = _dot(c_g, st_g.astype(BF16)) * decay_in[:, gs]
            y_pairs = []
            for hp in range(heads_per_group // 2):
                ms = []
                for h in (g * heads_per_group + 2 * hp, g * heads_per_group + 2 * hp + 1):
                    seg = jnp.broadcast_to(acs_k[:, h:h + 1], (q, q)) - acs_r[h:h + 1, rows]
                    dec = jnp.exp(jnp.where(tril, seg, -1e30))
                    ms.append((cb * dec * dt_r[h:h + 1, rows]).astype(BF16))
                pair = xs_bf[:, g * grp_w + hp * LANES:g * grp_w + (hp + 1) * LANES]
                zero = jnp.zeros_like(pair)
                rhs = jnp.concatenate([jnp.where(left, pair, zero), jnp.where(left, zero, pair)], axis=0)
                y_pairs.append(_dot(jnp.concatenate(ms, axis=1), rhs))
            y_g = y_g + jnp.concatenate(y_pairs, axis=1) + dskip_ref[:, gs] * xs[:, gs]
            y_g = y_g * _silu(zk[:, gs])
            y_g = y_g * lax.rsqrt(jnp.mean(y_g * y_g, axis=-1, keepdims=True) + EPS)
            y_mix[rows, gs] = (y_g * ng_ref[:, gs]).astype(BF16)
            upd = lax.dot_general(b_g, xw_bf[:, gs], (((0,), (0,)), ((), ())), preferred_element_type=F32)
            state[:, gs] = st_g * decay_in[q - 1:q, gs] + upd

    uc = u_conv[...]
    mu = jnp.mean(uc, axis=-1, keepdims=True)
    ucc = uc - mu
    var = jnp.mean(ucc * ucc, axis=-1, keepdims=True)
    y_mix[:, d_ssm:d_ssm + d_conv] = _silu(
        ucc * lax.rsqrt(var + EPS) * lng_ref[...] + lnb_ref[...]).astype(BF16)

    m = _dot(y_mix[...], w_out_ref[...])
    o_ref[...] = x + _rms(m, post_g_ref[...])


def _mixer(x, pre_g, w_in, b_glu, ssd_conv_w, ssd_conv_b, dt_bias, a_log, d_skip, ssd_norm_g,
           conf_dw_w, conf_dw_b, conf_ln_g, conf_ln_b, w_out, post_g):
    b, seq, d = x.shape
    n_heads = a_log.shape[0]
    d_ssm = n_heads * SSM_HEAD_DIM
    d_conv = conf_dw_w.shape[1]
    gn = SSM_GROUPS * SSM_STATE
    d_xbc = d_ssm + 2 * gn
    assert n_heads <= LANES and SSD_Q == LANES and MIX_ROWS % SSD_Q == 0 and seq % MIX_ROWS == 0

    def row(v):
        return v.reshape(1, -1)

    def pad_lanes(v):
        return jnp.pad(v, (0, LANES - v.shape[0])).reshape(1, LANES)

    o_dt = d_ssm + d_xbc
    w_in_k = jnp.concatenate(
        [w_in[:, :o_dt], jnp.pad(w_in[:, o_dt:o_dt + n_heads], ((0, 0), (0, LANES - n_heads))),
         w_in[:, o_dt + n_heads:]], axis=1).astype(BF16)
    d_in_k = w_in_k.shape[1]
    expand = (jnp.arange(LANES)[:, None] == (jnp.arange(d_ssm) // SSM_HEAD_DIM)[None, :]).astype(BF16)
    expand = jnp.concatenate([expand, expand], axis=0)
    dskip_x = jnp.repeat(d_skip, SSM_HEAD_DIM).reshape(1, d_ssm)

    tile = pl.BlockSpec((None, MIX_ROWS, d), lambda bi, ti: (bi, ti, 0))
    kern = functools.partial(_mixer_kernel, d_ssm=d_ssm, d_conv=d_conv, n_heads=n_heads)
    return pl.pallas_call(
        kern,
        grid=(b, seq // MIX_ROWS),
        in_specs=[tile, _resident((1, d)), _resident((d, d_in_k)), _resident((1, 2 * d_conv)),
                  _resident((SSM_CONV_W, d_xbc)), _resident((1, d_xbc)), _resident((1, LANES)),
                  _resident((1, LANES)), _resident((1, d_ssm)), _resident((1, d_ssm)),
                  _resident((CONF_KERNEL, d_conv)), _resident((1, d_conv)), _resident((1, d_conv)),
                  _resident((1, d_conv)), _resident((2 * LANES, d_ssm)), _resident((d_ssm + d_conv, d)),
                  _resident((1, d))],
        out_specs=tile,
        out_shape=jax.ShapeDtypeStruct((b, seq, d), F32),
        scratch_shapes=[
            pltpu.VMEM((XBC_HALO + MIX_ROWS, d_xbc), F32),
            pltpu.VMEM((MIX_ROWS, d_xbc), F32),
            pltpu.VMEM((U_HALO + MIX_ROWS, d_conv), F32),
            pltpu.VMEM((MIX_ROWS, d_conv), F32),
            pltpu.VMEM((SSM_STATE, d_ssm), F32),
            pltpu.VMEM((MIX_ROWS, d_ssm + d_conv), BF16),
        ],
        compiler_params=pltpu.CompilerParams(
            dimension_semantics=("arbitrary", "arbitrary"), vmem_limit_bytes=VMEM_LIMIT_BYTES),
    )(x, row(pre_g), w_in_k, row(b_glu), ssd_conv_w, row(ssd_conv_b), pad_lanes(dt_bias),
      pad_lanes(a_log), dskip_x, row(ssd_norm_g), conf_dw_w, row(conf_dw_b), row(conf_ln_g),
      row(conf_ln_b), expand, w_out.astype(BF16), row(post_g))


def kernel(x, ffn1_pre_g, ffn1_w_gate, ffn1_w_up, ffn1_w_down, ffn1_post_g, mix_pre_g, w_in, b_glu, ssd_conv_w, ssd_conv_b, dt_bias, a_log, d_skip, ssd_norm_g, conf_dw_w, conf_dw_b, conf_ln_g, conf_ln_b, w_out, mix_post_g, ffn2_pre_g, ffn2_w_gate, ffn2_w_up, ffn2_w_down, ffn2_post_g):
    b, seq, d = x.shape
    for i in range(ffn1_pre_g.shape[0]):
        x = _ffn(x.reshape(b * seq, d), ffn1_pre_g[i], ffn1_w_gate[i], ffn1_w_up[i], ffn1_w_down[i],
                 ffn1_post_g[i]).reshape(b, seq, d)
        x = _mixer(x, mix_pre_g[i], w_in[i], b_glu[i], ssd_conv_w[i], ssd_conv_b[i], dt_bias[i],
                   a_log[i], d_skip[i], ssd_norm_g[i], conf_dw_w[i], conf_dw_b[i], conf_ln_g[i],
                   conf_ln_b[i], w_out[i], mix_post_g[i])
        x = _ffn(x.reshape(b * seq, d), ffn2_pre_g[i], ffn2_w_gate[i], ffn2_w_up[i], ffn2_w_down[i],
                 ffn2_post_g[i]).reshape(b, seq, d)
    return x
```
